```python
import math
import jax, jax.numpy as jnp
from jax import lax
import numpy as np

D_MODEL = 1024
BATCH = 8
SEQ = 2048
DEPTH = 1
DEC_BATCH = 128
DEC_SEQ = 8
PAST_LEN = 16384
PAGE_SIZE = 128

S5_WIDTH = D_MODEL // 2
S5_GROUP = 16
S5_GROUPS = S5_WIDTH // S5_GROUP
S5_STATE = 64
DT_MIN = 1e-3
DT_MAX = 1e-1
POOL_WIDTH = D_MODEL // 2
POOL_WINDOWS = (2, 4, 8, 16)
POOL_GROUPS = len(POOL_WINDOWS)
POOL_GROUP_W = POOL_WIDTH // POOL_GROUPS
POOL_OUT_W = D_MODEL // POOL_GROUPS
POOL_BUF = max(POOL_WINDOWS) - 1
N_BRANCH = 2
IN_WIDTH = S5_WIDTH + POOL_WIDTH + N_BRANCH * D_MODEL
D_FF = 2816
EPS = 1e-6

kernel_name = "s5_pool_gated_macaron_decoder_step"


def rmsnorm(x, g):
    xf = x.astype(jnp.float32)
    y = xf * lax.rsqrt(jnp.mean(xf * xf, axis=-1, keepdims=True) + EPS) * g.astype(jnp.float32)
    return y.astype(x.dtype)


def swiglu(x, w_up, w_down):
    gu = jnp.matmul(x, w_up)
    g, u = jnp.split(gu, 2, axis=-1)
    return jnp.matmul(jax.nn.silu(g) * u, w_down)


def _complex_affine_combine(e1, e2):
    a1r, a1i, b1r, b1i = e1
    a2r, a2i, b2r, b2i = e2
    return (a2r * a1r - a2i * a1i,
            a2r * a1i + a2i * a1r,
            a2r * b1r - a2i * b1i + b2r,
            a2r * b1i + a2i * b1r + b2i)


def s5_branch(u, h0_re, h0_im, lam_re, lam_im, log_step, b_re, b_im, c_re, c_im, d_skip, w_glu):
    B, L, _ = u.shape
    uf = u.astype(jnp.float32).reshape(B, L, S5_GROUPS, S5_GROUP)
    lr = lam_re.astype(jnp.float32)
    li = lam_im.astype(jnp.float32)
    dt = jnp.exp(log_step.astype(jnp.float32))[:, None]
    mag = jnp.exp(lr * dt)
    ar = mag * jnp.cos(li * dt)
    ai = mag * jnp.sin(li * dt)
    den = lr * lr + li * li
    nr = ar - 1.0
    ni = ai
    cr = (nr * lr + ni * li) / den
    ci = (ni * lr - nr * li) / den
    br = b_re.astype(jnp.float32)
    bi = b_im.astype(jnp.float32)
    bbr = cr[..., None] * br - ci[..., None] * bi
    bbi = cr[..., None] * bi + ci[..., None] * br
    bu_r = jnp.einsum('blgh,gph->blgp', uf, bbr)
    bu_i = jnp.einsum('blgh,gph->blgp', uf, bbi)
    a_r = jnp.broadcast_to(ar, bu_r.shape)
    a_i = jnp.broadcast_to(ai, bu_i.shape)
    A_r, A_i, h_r, h_i = lax.associative_scan(_complex_affine_combine, (a_r, a_i, bu_r, bu_i), axis=1)
    h0r = h0_re.astype(jnp.float32)[:, None]
    h0i = h0_im.astype(jnp.float32)[:, None]
    h_r, h_i = (h_r + A_r * h0r - A_i * h0i,
                h_i + A_r * h0i + A_i * h0r)
    y = (jnp.einsum('gqp,blgp->blgq', c_re.astype(jnp.float32), h_r)
         - jnp.einsum('gqp,blgp->blgq', c_im.astype(jnp.float32), h_i)
         + d_skip.astype(jnp.float32).reshape(S5_GROUPS, S5_GROUP) * uf)
    y = jax.nn.gelu(y.reshape(B, L, S5_WIDTH))
    ab = jnp.matmul(y, w_glu.astype(jnp.float32))
    a, b = jnp.split(ab, 2, axis=-1)
    return a * jax.nn.sigmoid(b), h_r[:, -1], h_i[:, -1]


def pool_branch(v, buf, pos0, w_pool, pool_scale):
    B, L, C = v.shape
    vf = v.astype(jnp.float32)
    padded = jnp.concatenate([buf.astype(jnp.float32), vf], axis=1)
    cs = jnp.concatenate([jnp.zeros((B, 1, C), jnp.float32), jnp.cumsum(padded, axis=1)], axis=1)
    pos = pos0 + jnp.arange(L, dtype=jnp.float32)
    outs = []
    for g, w in enumerate(POOL_WINDOWS):
        sl = slice(g * POOL_GROUP_W, (g + 1) * POOL_GROUP_W)
        s = cs[:, POOL_BUF + 1:POOL_BUF + 1 + L, sl] - cs[:, POOL_BUF + 1 - w:POOL_BUF + 1 - w + L, sl]
        cnt = jnp.minimum(float(w), pos + 1.0)[None, :, None]
        outs.append(s / cnt - vf[..., sl])
    pooled = jnp.stack(outs, axis=2)
    y = jnp.einsum('blgc,gcd->blgd', pooled, w_pool.astype(jnp.float32)).reshape(B, L, D_MODEL)
    y = y * pool_scale.astype(jnp.float32)
    new_buf = padded[:, -POOL_BUF:].astype(buf.dtype)
    return y, new_buf


def block(x, h0_re, h0_im, pool_buf, pos0, ffn1_norm, ffn1_up, ffn1_down, mix_norm, w_in,
          lam_re, lam_im, log_step, b_re, b_im, c_re, c_im, d_skip, w_glu, w_pool, pool_scale,
          w_out, ffn2_norm, ffn2_up, ffn2_down):
    B, L, _ = x.shape
    x = x + (0.5 * swiglu(rmsnorm(x, ffn1_norm), ffn1_up, ffn1_down)).astype(x.dtype)
    hn = rmsnorm(x, mix_norm)
    z = jnp.matmul(hn, w_in)
    u_s = z[..., :S5_WIDTH]
    u_p = z[..., S5_WIDTH:S5_WIDTH + POOL_WIDTH]
    gates = jax.nn.sigmoid(z[..., S5_WIDTH + POOL_WIDTH:].astype(jnp.float32)).reshape(B, L, N_BRANCH, D_MODEL)
    y_s, h_re, h_im = s5_branch(u_s, h0_re, h0_im, lam_re, lam_im, log_step, b_re, b_im, c_re, c_im, d_skip, w_glu)
    y_p, new_buf = pool_branch(u_p, pool_buf, pos0, w_pool, pool_scale)
    merged = gates[:, :, 0] * y_s + gates[:, :, 1] * y_p
    x = x + jnp.matmul(merged, w_out.astype(jnp.float32)).astype(x.dtype)
    x = x + (0.5 * swiglu(rmsnorm(x, ffn2_norm), ffn2_up, ffn2_down)).astype(x.dtype)
    return x, h_re.astype(h0_re.dtype), h_im.astype(h0_im.dtype), new_buf


def setup_inputs(seed: int = 0) -> dict:
    key = jax.random.key(seed)
    ks = jax.random.split(key, 32)
    f32 = jnp.float32
    nrm = lambda k, shape, s: jax.random.normal(k, shape, f32) * s
    d = {}
    d["x_prompt"] = nrm(ks[0], (BATCH, SEQ, D_MODEL), 1.0)
    d["x_sample"] = nrm(ks[1], (DEC_BATCH, DEC_SEQ, D_MODEL), 1.0)
    d["state_ssm_re"] = nrm(ks[2], (DEPTH, DEC_BATCH, S5_GROUPS, S5_STATE), 0.5)
    d["state_ssm_im"] = nrm(ks[3], (DEPTH, DEC_BATCH, S5_GROUPS, S5_STATE), 0.5)
    d["state_pool"] = nrm(ks[4], (DEPTH, DEC_BATCH, POOL_BUF, POOL_WIDTH), 1.0)
    d["ffn1_norm"] = 1.0 + nrm(ks[5], (DEPTH, D_MODEL), 0.01)
    d["ffn1_up"] = nrm(ks[6], (DEPTH, D_MODEL, 2 * D_FF), D_MODEL ** -0.5)
    d["ffn1_down"] = nrm(ks[7], (DEPTH, D_FF, D_MODEL), D_FF ** -0.5)
    d["mix_norm"] = 1.0 + nrm(ks[8], (DEPTH, D_MODEL), 0.01)
    d["w_in"] = nrm(ks[9], (DEPTH, D_MODEL, IN_WIDTH), D_MODEL ** -0.5)
    d["lam_re"] = -0.5 + nrm(ks[10], (DEPTH, S5_GROUPS, S5_STATE), 0.01)
    d["lam_im"] = math.pi * jnp.arange(S5_STATE, dtype=f32) + nrm(ks[11], (DEPTH, S5_GROUPS, S5_STATE), 0.01)
    d["log_step"] = jax.random.uniform(ks[12], (DEPTH, S5_GROUPS), f32, math.log(DT_MIN), math.log(DT_MAX))
    d["b_re"] = nrm(ks[13], (DEPTH, S5_GROUPS, S5_STATE, S5_GROUP), (2 * S5_GROUP) ** -0.5)
    d["b_im"] = nrm(ks[14], (DEPTH, S5_GROUPS, S5_STATE, S5_GROUP), (2 * S5_GROUP) ** -0.5)
    d["c_re"] = nrm(ks[15], (DEPTH, S5_GROUPS, S5_GROUP, S5_STATE), (2 * S5_STATE) ** -0.5)
    d["c_im"] = nrm(ks[16], (DEPTH, S5_GROUPS, S5_GROUP, S5_STATE), (2 * S5_STATE) ** -0.5)
    d["d_skip"] = nrm(ks[17], (DEPTH, S5_WIDTH), 1.0)
    d["w_glu"] = nrm(ks[18], (DEPTH, S5_WIDTH, 2 * D_MODEL), S5_WIDTH ** -0.5)
    d["w_pool"] = nrm(ks[19], (DEPTH, POOL_GROUPS, POOL_GROUP_W, POOL_OUT_W), POOL_GROUP_W ** -0.5)
    d["pool_scale"] = 1.0 + nrm(ks[20], (DEPTH, D_MODEL), 0.1)
    d["w_out"] = nrm(ks[21], (DEPTH, D_MODEL, D_MODEL), D_MODEL ** -0.5)
    d["ffn2_norm"] = 1.0 + nrm(ks[22], (DEPTH, D_MODEL), 0.01)
    d["ffn2_up"] = nrm(ks[23], (DEPTH, D_MODEL, 2 * D_FF), D_MODEL ** -0.5)
    d["ffn2_down"] = nrm(ks[24], (DEPTH, D_FF, D_MODEL), D_FF ** -0.5)
    d["final_norm"] = 1.0 + nrm(ks[25], (D_MODEL,), 0.01)
    return d


def reference(x_prompt, x_sample, state_ssm_re, state_ssm_im, state_pool,
              ffn1_norm, ffn1_up, ffn1_down, mix_norm, w_in, lam_re, lam_im, log_step,
              b_re, b_im, c_re, c_im, d_skip, w_glu, w_pool, pool_scale, w_out,
              ffn2_norm, ffn2_up, ffn2_down, final_norm):
    xp = x_prompt
    xs = x_sample
    Bp = x_prompt.shape[0]
    zero_h = jnp.zeros((Bp, S5_GROUPS, S5_STATE), state_ssm_re.dtype)
    zero_buf = jnp.zeros((Bp, POOL_BUF, POOL_WIDTH), state_pool.dtype)
    p_re, p_im, p_buf, s_re, s_im, s_buf = [], [], [], [], [], []
    for l in range(DEPTH):
        w = (ffn1_norm[l], ffn1_up[l], ffn1_down[l], mix_norm[l], w_in[l], lam_re[l], lam_im[l],
             log_step[l], b_re[l], b_im[l], c_re[l], c_im[l], d_skip[l], w_glu[l], w_pool[l],
             pool_scale[l], w_out[l], ffn2_norm[l], ffn2_up[l], ffn2_down[l])
        xp, hr, hi, nb = block(xp, zero_h, zero_h, zero_buf, 0, *w)
        p_re.append(hr); p_im.append(hi); p_buf.append(nb)
        xs, hr, hi, nb = block(xs, state_ssm_re[l], state_ssm_im[l], state_pool[l], PAST_LEN, *w)
        s_re.append(hr); s_im.append(hi); s_buf.append(nb)
    y_prompt = rmsnorm(xp, final_norm)
    y_sample = rmsnorm(xs, final_norm)
    return (y_prompt, y_sample,
            jnp.stack(p_re), jnp.stack(p_im), jnp.stack(p_buf),
            jnp.stack(s_re), jnp.stack(s_im), jnp.stack(s_buf))
```

```python
import functools

import jax
import jax.numpy as jnp
from jax import lax
from jax.experimental import pallas as pl
from jax.experimental.pallas import tpu as pltpu

D_MODEL = 1024
D_FF = 2816
S5_WIDTH = 512
S5_GROUPS = 32
S5_GROUP = 16
S5_STATE = 64
POOL_WIDTH = 512
POOL_WINDOWS = (2, 4, 8, 16)
POOL_GROUP_W = 128
POOL_OUT_W = 256
POOL_BUF = 15
EPS = 1e-6

LANES = 128
CHUNK = 8
N_SUPER = S5_WIDTH // LANES
SUPER_STATES = S5_GROUPS // N_SUPER * S5_STATE
HALO = 16
FF_CHUNKS = ((0, 1024), (1024, 1024), (2048, 768))
VMEM_LIMIT = 56 * 1024 * 1024

F32 = jnp.float32
BF16 = jnp.bfloat16


def _dot(a, b):
    return jnp.dot(a, b, preferred_element_type=F32)


def _rmsnorm(x, g):
    return x * lax.rsqrt(jnp.mean(x * x, axis=-1, keepdims=True) + EPS) * g


def _swiglu(xn, up_ref, down_ref):
    acc = None
    for start, size in FF_CHUNKS:
        g = _dot(xn, up_ref[:, start:start + size])
        u = _dot(xn, up_ref[:, D_FF + start:D_FF + start + size])
        act = (g * jax.nn.sigmoid(g) * u).astype(BF16)
        part = _dot(act, down_ref[start:start + size, :])
        acc = part if acc is None else acc + part
    return acc


def _discretise(lr, li, ls):
    dt = jnp.exp(ls)
    mag = jnp.exp(lr * dt)
    ar = mag * jnp.cos(li * dt)
    ai = mag * jnp.sin(li * dt)
    den = lr * lr + li * li
    nr = ar - 1.0
    ni = ai
    cr = (nr * lr + ni * li) / den
    ci = (ni * lr - nr * li) / den
    return ar, ai, cr, ci


def _powers(ar, ai, n):
    out = [(jnp.ones_like(ar), jnp.zeros_like(ai))]
    for _ in range(n):
        pr, pi = out[-1]
        out.append((pr * ar - pi * ai, pr * ai + pi * ar))
    return out


def _prep_kernel(lr_row, li_row, ls_row, lr_col, li_col, ls_col,
                 b_re, b_im, c_re, c_im, w1_ref, wy_ref, a8r_ref, a8i_ref):
    ar, ai, cr, ci = _discretise(lr_row[...], li_row[...], ls_row[...])
    pw = _powers(ar, ai, CHUNK)
    a8r_ref[...] = pw[CHUNK][0]
    a8i_ref[...] = pw[CHUNK][1]
    bre = b_re[...]
    bim = b_im[...]
    bbr = cr * bre - ci * bim
    bbi = cr * bim + ci * bre
    cre = c_re[...]
    cim = c_im[...]
    c_cat = jnp.concatenate([cre, -cim], axis=0)
    e = []
    for t in range(CHUNK):
        pr, pi = pw[t]
        e.append((bbr * pr - bbi * pi, bbr * pi + bbi * pr))
    zero_blk = jnp.zeros((LANES, LANES), BF16)
    kt = []
    for t in range(CHUNK):
        k = jnp.dot(jnp.concatenate(e[t], axis=1), c_cat,
                    precision=lax.Precision.HIGHEST, preferred_element_type=F32)
        kt.append(k.astype(BF16))
    for i in range(CHUNK):
        r = slice(i * LANES, (i + 1) * LANES)
        for j in range(CHUNK):
            w1_ref[r, j * LANES:(j + 1) * LANES] = kt[j - i] if j >= i else zero_blk
        er, ei = e[CHUNK - 1 - i]
        w1_ref[r, CHUNK * LANES:CHUNK * LANES + SUPER_STATES] = er.astype(BF16)
        w1_ref[r, CHUNK * LANES + SUPER_STATES:] = ei.astype(BF16)
    arc, aic, _, _ = _discretise(lr_col[...], li_col[...], ls_col[...])
    pwc = _powers(arc, aic, CHUNK)
    for j in range(CHUNK):
        pr, pi = pwc[j + 1]
        cols = slice(j * LANES, (j + 1) * LANES)
        wy_ref[0:SUPER_STATES, cols] = (cre * pr - cim * pi).astype(BF16)
        wy_ref[SUPER_STATES:, cols] = (-(cre * pi + cim * pr)).astype(BF16)


def _prep(lam_re, lam_im, log_step, b_re, b_im, c_re, c_im):
    gl = S5_GROUPS // N_SUPER
    eye = jnp.eye(gl, dtype=bool)

    def rows(v):
        return v.reshape(N_SUPER, 1, SUPER_STATES)

    def cols(v):
        return v.reshape(N_SUPER, SUPER_STATES, 1)

    ls = jnp.broadcast_to(log_step[:, None], (S5_GROUPS, S5_STATE))
    b4r = b_re.reshape(N_SUPER, gl, S5_STATE, S5_GROUP)
    b4i = b_im.reshape(N_SUPER, gl, S5_STATE, S5_GROUP)
    c4r = c_re.reshape(N_SUPER, gl, S5_GROUP, S5_STATE)
    c4i = c_im.reshape(N_SUPER, gl, S5_GROUP, S5_STATE)

    def bblk(b4):
        t = jnp.transpose(b4, (0, 1, 3, 2))[:, :, :, None, :]
        t = jnp.where(eye[None, :, None, :, None], t, 0.0)
        return t.reshape(N_SUPER, LANES, SUPER_STATES)

    def cblk(c4):
        t = jnp.transpose(c4, (0, 1, 3, 2))[:, :, :, None, :]
        t = jnp.where(eye[None, :, None, :, None], t, 0.0)
        return t.reshape(N_SUPER, SUPER_STATES, LANES)

    row_spec = pl.BlockSpec((None, 1, SUPER_STATES), lambda s: (s, 0, 0))
    col_spec = pl.BlockSpec((None, SUPER_STATES, 1), lambda s: (s, 0, 0))
    b_spec = pl.BlockSpec((None, LANES, SUPER_STATES), lambda s: (s, 0, 0))
    c_spec = pl.BlockSpec((None, SUPER_STATES, LANES), lambda s: (s, 0, 0))
    k = CHUNK * LANES
    return pl.pallas_call(
        _prep_kernel,
        grid=(N_SUPER,),
        in_specs=[row_spec] * 3 + [col_spec] * 3 + [b_spec] * 2 + [c_spec] * 2,
        out_specs=[
            pl.BlockSpec((None, k, k + 2 * SUPER_STATES), lambda s: (s, 0, 0)),
            pl.BlockSpec((None, 2 * SUPER_STATES, k), lambda s: (s, 0, 0)),
            row_spec, row_spec,
        ],
        out_shape=[
            jax.ShapeDtypeStruct((N_SUPER, k, k + 2 * SUPER_STATES), BF16),
            jax.ShapeDtypeStruct((N_SUPER, 2 * SUPER_STATES, k), BF16),
            jax.ShapeDtypeStruct((N_SUPER, 1, SUPER_STATES), F32),
            jax.ShapeDtypeStruct((N_SUPER, 1, SUPER_STATES), F32),
        ],
        compiler_params=pltpu.CompilerParams(vmem_limit_bytes=VMEM_LIMIT),
        name="s5_prep",
    )(rows(lam_re), rows(lam_im), rows(ls), cols(lam_re), cols(lam_im), cols(ls),
      bblk(b4r), bblk(b4i), cblk(c4r), cblk(c4i))


def _stage_a_kernel(x_ref, n1_ref, up_ref, down_ref, mixn_ref, wuv_ref,
                    x1_ref, us_ref, upool_ref):
    x = x_ref[...]
    xn = _rmsnorm(x, n1_ref[...]).astype(BF16)
    x1 = x + 0.5 * _swiglu(xn, up_ref, down_ref)
    x1_ref[...] = x1
    hn = _rmsnorm(x1, mixn_ref[...]).astype(BF16)
    z = _dot(hn, wuv_ref[...])
    for s in range(N_SUPER):
        us_ref[s] = z[:, s * LANES:(s + 1) * LANES]
    upool_ref[...] = z[:, S5_WIDTH:]


def _const_spec(shape):
    zeros = (0,) * len(shape)
    return pl.BlockSpec(shape, lambda *_: zeros, pipeline_mode=pl.Buffered(1))


def _stage_a(x, n1, up, down, mixn, wuv, tm):
    n = x.shape[0]
    return pl.pallas_call(
        _stage_a_kernel,
        grid=(n // tm,),
        in_specs=[
            pl.BlockSpec((tm, D_MODEL), lambda i: (i, 0)),
            _const_spec((1, D_MODEL)),
            _const_spec((D_MODEL, 2 * D_FF)),
            _const_spec((D_FF, D_MODEL)),
            _const_spec((1, D_MODEL)),
            _const_spec((D_MODEL, S5_WIDTH + POOL_WIDTH)),
        ],
        out_specs=[
            pl.BlockSpec((tm, D_MODEL), lambda i: (i, 0)),
            pl.BlockSpec((N_SUPER, tm, LANES), lambda i: (0, i, 0)),
            pl.BlockSpec((tm, POOL_WIDTH), lambda i: (i, 0)),
        ],
        out_shape=[
            jax.ShapeDtypeStruct((n, D_MODEL), F32),
            jax.ShapeDtypeStruct((N_SUPER, n, LANES), F32),
            jax.ShapeDtypeStruct((n, POOL_WIDTH), F32),
        ],
        compiler_params=pltpu.CompilerParams(
            dimension_semantics=("arbitrary",), vmem_limit_bytes=VMEM_LIMIT),
        name="stage_a",
    )(x, n1, up, down, mixn, wuv)


def _stage_b_scan_kernel(u_ref, w1_ref, wy_ref, a8r_ref, a8i_ref,
                         y_ref, hr_out, hi_out, x_scr, hin_scr):
    rows = u_ref.shape[0]
    k = CHUNK * LANES
    res = _dot(u_ref[...].astype(BF16), w1_ref[...])
    y_ref[...] = res[:, :k]
    x_scr[...] = res[:, k:]
    a8r = a8r_ref[...]
    a8i = a8i_ref[...]

    def step(c, carry):
        hr, hi = carry
        hin_scr[pl.ds(c, 1), 0:SUPER_STATES] = hr
        hin_scr[pl.ds(c, 1), SUPER_STATES:] = hi
        xr = x_scr[pl.ds(c, 1), 0:SUPER_STATES]
        xi = x_scr[pl.ds(c, 1), SUPER_STATES:]
        return (a8r * hr - a8i * hi + xr, a8r * hi + a8i * hr + xi)

    zero = jnp.zeros((1, SUPER_STATES), F32)
    hr, hi = lax.fori_loop(0, rows, step, (zero, zero), unroll=8)
    hr_out[...] = hr
    hi_out[...] = hi
    y_ref[...] += _dot(hin_scr[...].astype(BF16), wy_ref[...])


def _stage_b_step_kernel(u_ref, hr_ref, hi_ref, w1_ref, wy_ref, a8r_ref, a8i_ref,
                         y_ref, hr_out, hi_out):
    k = CHUNK * LANES
    res = _dot(u_ref[...].astype(BF16), w1_ref[...])
    hr = hr_ref[...]
    hi = hi_ref[...]
    hin = jnp.concatenate([hr, hi], axis=1).astype(BF16)
    y_ref[...] = res[:, :k] + _dot(hin, wy_ref[...])
    a8r = a8r_ref[...]
    a8i = a8i_ref[...]
    hr_out[...] = a8r * hr - a8i * hi + res[:, k:k + SUPER_STATES]
    hi_out[...] = a8r * hi + a8i * hr + res[:, k + SUPER_STATES:]


def _stage_b_specs():
    k = CHUNK * LANES
    w1 = pl.BlockSpec((None, k, k + 2 * SUPER_STATES), lambda s, *_: (s, 0, 0))
    wy = pl.BlockSpec((None, 2 * SUPER_STATES, k), lambda s, *_: (s, 0, 0))
    a8 = pl.BlockSpec((None, 1, SUPER_STATES), lambda s, *_: (s, 0, 0))
    return w1, wy, a8


def _stage_b_prompt(us4, w1, wy, a8r, a8i, batch, chunks):
    k = CHUNK * LANES
    w1s, wys, a8s = _stage_b_specs()
    st = pl.BlockSpec((None, 1, SUPER_STATES), lambda s, b: (b, 0, s))
    return pl.pallas_call(
        _stage_b_scan_kernel,
        grid=(N_SUPER, batch),
        in_specs=[pl.BlockSpec((None, chunks, k), lambda s, b: (s, b, 0)), w1s, wys, a8s, a8s],
        out_specs=[pl.BlockSpec((None, chunks, k), lambda s, b: (s, b, 0)), st, st],
        out_shape=[
            jax.ShapeDtypeStruct((N_SUPER, batch * chunks, k), F32),
            jax.ShapeDtypeStruct((batch, 1, N_SUPER * SUPER_STATES), F32),
            jax.ShapeDtypeStruct((batch, 1, N_SUPER * SUPER_STATES), F32),
        ],
        scratch_shapes=[pltpu.VMEM((chunks, 2 * SUPER_STATES), F32),
                        pltpu.VMEM((chunks, 2 * SUPER_STATES), F32)],
        compiler_params=pltpu.CompilerParams(
            dimension_semantics=("arbitrary", "arbitrary"), vmem_limit_bytes=VMEM_LIMIT),
        name="stage_b_prompt",
    )(us4, w1, wy, a8r, a8i)


def _stage_b_sample(us4, h_re, h_im, w1, wy, a8r, a8i):
    rows = us4.shape[1]
    k = CHUNK * LANES
    w1s, wys, a8s = _stage_b_specs()
    st = pl.BlockSpec((rows, SUPER_STATES), lambda s: (0, s))
    u = pl.BlockSpec((None, rows, k), lambda s: (s, 0, 0))
    return pl.pallas_call(
        _stage_b_step_kernel,
        grid=(N_SUPER,),
        in_specs=[u, st, st, w1s, wys, a8s, a8s],
        out_specs=[u, st, st],
        out_shape=[
            jax.ShapeDtypeStruct((N_SUPER, rows, k), F32),
            jax.ShapeDtypeStruct((rows, N_SUPER * SUPER_STATES), F32),
            jax.ShapeDtypeStruct((rows, N_SUPER * SUPER_STATES), F32),
        ],
        compiler_params=pltpu.CompilerParams(
            dimension_semantics=("arbitrary",), vmem_limit_bytes=VMEM_LIMIT),
        name="stage_b_sample",
    )(us4, h_re, h_im, w1, wy, a8r, a8i)


def _mixer_tail(x1, us, ys_pre, pooled, mixn_ref, wg_ref, dskip_ref, wglu_ref, wpool_ref,
                pscale_ref, wout_ref, n2_ref, up_ref, down_ref, fnorm_ref):
    hn = _rmsnorm(x1, mixn_ref[...]).astype(BF16)
    gates = jax.nn.sigmoid(_dot(hn, wg_ref[...]))
    ys = jax.nn.gelu(ys_pre + dskip_ref[...] * us)
    ab = _dot(ys.astype(BF16), wglu_ref[...])
    y_s = ab[:, :D_MODEL] * jax.nn.sigmoid(ab[:, D_MODEL:])
    y_p = jnp.concatenate(
        [_dot(pooled[g].astype(BF16), wpool_ref[g]) for g in range(len(POOL_WINDOWS))], axis=1)
    y_p = y_p * pscale_ref[...]
    merged = gates[:, :D_MODEL] * y_s + gates[:, D_MODEL:] * y_p
    x2 = x1 + _dot(merged.astype(BF16), wout_ref[...])
    xn = _rmsnorm(x2, n2_ref[...]).astype(BF16)
    x3 = x2 + 0.5 * _swiglu(xn, up_ref, down_ref)
    return _rmsnorm(x3, fnorm_ref[...])


def _stage_c_prompt_kernel(tiles_per_seq, x1_ref, us_ref, y_ref, upool_ref, halo_ref, *rest):
    weights, out_ref, pad_scr = rest[:-2], rest[-2], rest[-1]
    tm = x1_ref.shape[0]
    t_in_seq = pl.program_id(0) % tiles_per_seq
    keep = (t_in_seq > 0).astype(F32)
    pad_scr[0:HALO, :] = halo_ref[...] * keep
    v = upool_ref[...]
    pad_scr[HALO:, :] = v
    pos = (t_in_seq * tm + lax.broadcasted_iota(jnp.int32, (tm, 1), 0)).astype(F32)
    pooled = []
    for g, w in enumerate(POOL_WINDOWS):
        lanes = slice(g * POOL_GROUP_W, (g + 1) * POOL_GROUP_W)
        s = pad_scr[HALO:HALO + tm, lanes]
        for k in range(1, w):
            s = s + pad_scr[HALO - k:HALO - k + tm, lanes]
        cnt = jnp.minimum(float(w), pos + 1.0)
        pooled.append(s / cnt - v[:, lanes])
    us = jnp.concatenate([us_ref[s] for s in range(N_SUPER)], axis=1)
    ys_pre = jnp.concatenate([y_ref[s] for s in range(N_SUPER)], axis=1)
    out_ref[...] = _mixer_tail(x1_ref[...], us, ys_pre, pooled, *weights)


def _stage_c_sample_kernel(x1_ref, us_ref, y_ref, upool_ref, buf_ref, *rest):
    weights, out_ref, pad_scr = rest[:-2], rest[-2], rest[-1]
    tm = x1_ref.shape[0]
    nb = tm // CHUNK
    pad_scr[:, 0:HALO, :] = buf_ref[...]
    v = upool_ref[...]
    pad_scr[:, HALO:, :] = v.reshape(nb, CHUNK, POOL_WIDTH)
    pooled = []
    for g, w in enumerate(POOL_WINDOWS):
        lanes = slice(g * POOL_GROUP_W, (g + 1) * POOL_GROUP_W)
        s = pad_scr[:, HALO:HALO + CHUNK, lanes]
        for k in range(1, w):
            s = s + pad_scr[:, HALO - k:HALO - k + CHUNK, lanes]
        pooled.append(s.reshape(tm, POOL_GROUP_W) / float(w) - v[:, lanes])
    us = jnp.concatenate([us_ref[s] for s in range(N_SUPER)], axis=1)
    ys_pre = jnp.concatenate([y_ref[s] for s in range(N_SUPER)], axis=1)
    out_ref[...] = _mixer_tail(x1_ref[...], us, ys_pre, pooled, *weights)


def _stage_c_weight_specs():
    return [
        _const_spec((1, D_MODEL)),
        _const_spec((D_MODEL, 2 * D_MODEL)),
        _const_spec((1, S5_WIDTH)),
        _const_spec((S5_WIDTH, 2 * D_MODEL)),
        _const_spec((len(POOL_WINDOWS), POOL_GROUP_W, POOL_OUT_W)),
        _const_spec((1, D_MODEL)),
        _const_spec((D_MODEL, D_MODEL)),
        _const_spec((1, D_MODEL)),
        _const_spec((D_MODEL, 2 * D_FF)),
        _const_spec((D_FF, D_MODEL)),
        _const_spec((1, D_MODEL)),
    ]


def _stage_c_prompt(x1, us4, y4, upool, weights, seq, tm):
    n = x1.shape[0]
    tiles_per_seq = seq // tm
    halo_blocks = tm // HALO
    row = pl.BlockSpec((tm, D_MODEL), lambda i: (i, 0))
    lane4 = pl.BlockSpec((N_SUPER, tm, LANES), lambda i: (0, i, 0))
    return pl.pallas_call(
        functools.partial(_stage_c_prompt_kernel, tiles_per_seq),
        grid=(n // tm,),
        in_specs=[row, lane4, lane4,
                  pl.BlockSpec((tm, POOL_WIDTH), lambda i: (i, 0)),
                  pl.BlockSpec((HALO, POOL_WIDTH),
                               lambda i: (jnp.maximum(i * halo_blocks - 1, 0), 0)),
                  ] + _stage_c_weight_specs(),
        out_specs=row,
        out_shape=jax.ShapeDtypeStruct((n, D_MODEL), F32),
        scratch_shapes=[pltpu.VMEM((HALO + tm, POOL_WIDTH), F32)],
        compiler_params=pltpu.CompilerParams(
            dimension_semantics=("arbitrary",), vmem_limit_bytes=VMEM_LIMIT),
        name="stage_c_prompt",
    )(x1, us4, y4, upool, upool, *weights)


def _stage_c_sample(x1, us4, y4, upool, buf16, weights, tm):
    n = x1.shape[0]
    nb = tm // CHUNK
    row = pl.BlockSpec((tm, D_MODEL), lambda i: (i, 0))
    lane4 = pl.BlockSpec((N_SUPER, tm, LANES), lambda i: (0, i, 0))
    return pl.pallas_call(
        _stage_c_sample_kernel,
        grid=(n // tm,),
        in_specs=[row, lane4, lane4,
                  pl.BlockSpec((tm, POOL_WIDTH), lambda i: (i, 0)),
                  pl.BlockSpec((nb, HALO, POOL_WIDTH), lambda i: (i, 0, 0)),
                  ] + _stage_c_weight_specs(),
        out_specs=row,
        out_shape=jax.ShapeDtypeStruct((n, D_MODEL), F32),
        scratch_shapes=[pltpu.VMEM((nb, HALO + CHUNK, POOL_WIDTH), F32)],
        compiler_params=pltpu.CompilerParams(
            dimension_semantics=("arbitrary",), vmem_limit_bytes=VMEM_LIMIT),
        name="stage_c_sample",
    )(x1, us4, y4, upool, buf16, *weights)


TM_A = 512
TM_C = 256


def kernel(x_prompt, x_sample, state_ssm_re, state_ssm_im, state_pool, ffn1_norm, ffn1_up,
           ffn1_down, mix_norm, w_in, lam_re, lam_im, log_step, b_re, b_im, c_re, c_im, d_skip,
           w_glu, w_pool, pool_scale, w_out, ffn2_norm, ffn2_up, ffn2_down, final_norm):
    depth = ffn1_norm.shape[0]
    assert depth == 1, "single-layer trunk"
    bp, seq, _ = x_prompt.shape
    bs, dseq, _ = x_sample.shape
    assert dseq == CHUNK and seq % CHUNK == 0
    l = 0
    row = lambda v: v.reshape(1, -1)
    n1, mixn, n2, fnorm = row(ffn1_norm[l]), row(mix_norm[l]), row(ffn2_norm[l]), row(final_norm)
    up1, down1 = ffn1_up[l].astype(BF16), ffn1_down[l].astype(BF16)
    up2, down2 = ffn2_up[l].astype(BF16), ffn2_down[l].astype(BF16)
    w_in_b = w_in[l].astype(BF16)
    wuv, wg = w_in_b[:, :S5_WIDTH + POOL_WIDTH], w_in_b[:, S5_WIDTH + POOL_WIDTH:]
    c_weights = (mixn, wg, row(d_skip[l]), w_glu[l].astype(BF16), w_pool[l].astype(BF16),
                 row(pool_scale[l]), w_out[l].astype(BF16), n2, up2, down2, fnorm)

    w1, wy, a8r, a8i = _prep(lam_re[l], lam_im[l], log_step[l], b_re[l], b_im[l], c_re[l], c_im[l])
    k = CHUNK * LANES

    xp = x_prompt.reshape(bp * seq, D_MODEL)
    x1p, us4p, upoolp = _stage_a(xp, n1, up1, down1, mixn, wuv, TM_A)
    chunks = seq // CHUNK
    y4p, hrp, hip = _stage_b_prompt(us4p.reshape(N_SUPER, bp * chunks, k), w1, wy, a8r, a8i,
                                    bp, chunks)
    yp = _stage_c_prompt(x1p, us4p, y4p.reshape(N_SUPER, bp * seq, LANES), upoolp, c_weights,
                         seq, TM_C)

    xs = x_sample.reshape(bs * dseq, D_MODEL)
    x1s, us4s, upools = _stage_a(xs, n1, up1, down1, mixn, wuv, TM_A)
    h_re = state_ssm_re[l].reshape(bs, S5_GROUPS * S5_STATE)
    h_im = state_ssm_im[l].reshape(bs, S5_GROUPS * S5_STATE)
    y4s, hrs, his = _stage_b_sample(us4s.reshape(N_SUPER, bs, k), h_re, h_im, w1, wy, a8r, a8i)
    buf16 = jnp.pad(state_pool[l], ((0, 0), (HALO - POOL_BUF, 0), (0, 0)))
    ys = _stage_c_sample(x1s, us4s, y4s.reshape(N_SUPER, bs * dseq, LANES), upools, buf16,
                         c_weights, TM_C)

    st_shape = lambda b: (depth, b, S5_GROUPS, S5_STATE)
    new_pool_p = upoolp.reshape(bp, seq, POOL_WIDTH)[:, seq - POOL_BUF:]
    new_pool_s = jnp.concatenate(
        [state_pool[l], upools.reshape(bs, dseq, POOL_WIDTH)], axis=1)[:, dseq:]
    return (yp.reshape(bp, seq, D_MODEL), ys.reshape(bs, dseq, D_MODEL),
            hrp.reshape(st_shape(bp)), hip.reshape(st_shape(bp)),
            new_pool_p[None].astype(state_pool.dtype),
            hrs.reshape(st_shape(bs)), his.reshape(st_shape(bs)),
            new_pool_s[None].astype(state_pool.dtype))
```

```python
import functools

import jax
import jax.numpy as jnp
from jax import lax
from jax.experimental import pallas as pl
from jax.experimental.pallas import tpu as pltpu

D_MODEL = 1024
D_FF = 2816
S5_WIDTH = 512
S5_GROUPS = 32
S5_GROUP = 16
S5_STATE = 64
POOL_WIDTH = 512
POOL_WINDOWS = (2, 4, 8, 16)
POOL_GROUP_W = 128
POOL_OUT_W = 256
POOL_BUF = 15
EPS = 1e-6

LANES = 128
SUBLANES = 8
CHUNK = 8
N_SUPER = S5_WIDTH // LANES
SUPER_STATES = S5_GROUPS // N_SUPER * S5_STATE
HALO = 16
FF_CHUNKS = ((0, 1024), (1024, 1024), (2048, 768))
VMEM_LIMIT = 56 * 1024 * 1024

F32 = jnp.float32
BF16 = jnp.bfloat16


def _dot(a, b):
    return jnp.dot(a, b, preferred_element_type=F32)


def _rmsnorm(x, g):
    return x * lax.rsqrt(jnp.mean(x * x, axis=-1, keepdims=True) + EPS) * g


def _swiglu(xn, up_ref, down_ref):
    acc = None
    for start, size in FF_CHUNKS:
        g = _dot(xn, up_ref[:, start:start + size])
        u = _dot(xn, up_ref[:, D_FF + start:D_FF + start + size])
        act = (g * jax.nn.sigmoid(g) * u).astype(BF16)
        part = _dot(act, down_ref[start:start + size, :])
        acc = part if acc is None else acc + part
    return acc


def _discretise(lr, li, ls):
    dt = jnp.exp(ls)
    mag = jnp.exp(lr * dt)
    ar = mag * jnp.cos(li * dt)
    ai = mag * jnp.sin(li * dt)
    den = lr * lr + li * li
    nr = ar - 1.0
    ni = ai
    cr = (nr * lr + ni * li) / den
    ci = (ni * lr - nr * li) / den
    return ar, ai, cr, ci


def _powers(ar, ai, n):
    out = [(jnp.ones_like(ar), jnp.zeros_like(ai))]
    for _ in range(n):
        pr, pi = out[-1]
        out.append((pr * ar - pi * ai, pr * ai + pi * ar))
    return out


def _prep_kernel(lr_row, li_row, ls_row, lr_col, li_col, ls_col,
                 b_re, b_im, c_re, c_im, w1_ref, wy_ref, a8r_ref, a8i_ref):
    ar, ai, cr, ci = _discretise(lr_row[...], li_row[...], ls_row[...])
    pw = _powers(ar, ai, CHUNK)
    a8r_ref[...] = pw[CHUNK][0]
    a8i_ref[...] = pw[CHUNK][1]
    bre = b_re[...]
    bim = b_im[...]
    bbr = cr * bre - ci * bim
    bbi = cr * bim + ci * bre
    cre = c_re[...]
    cim = c_im[...]
    c_cat = jnp.concatenate([cre, -cim], axis=0)
    e = []
    for t in range(CHUNK):
        pr, pi = pw[t]
        e.append((bbr * pr - bbi * pi, bbr * pi + bbi * pr))
    zero_blk = jnp.zeros((LANES, LANES), BF16)
    kt = []
    for t in range(CHUNK):
        k = jnp.dot(jnp.concatenate(e[t], axis=1), c_cat,
                    precision=lax.Precision.HIGHEST, preferred_element_type=F32)
        kt.append(k.astype(BF16))
    for i in range(CHUNK):
        r = slice(i * LANES, (i + 1) * LANES)
        for j in range(CHUNK):
            w1_ref[r, j * LANES:(j + 1) * LANES] = kt[j - i] if j >= i else zero_blk
        er, ei = e[CHUNK - 1 - i]
        w1_ref[r, CHUNK * LANES:CHUNK * LANES + SUPER_STATES] = er.astype(BF16)
        w1_ref[r, CHUNK * LANES + SUPER_STATES:] = ei.astype(BF16)
    arc, aic, _, _ = _discretise(lr_col[...], li_col[...], ls_col[...])
    pwc = _powers(arc, aic, CHUNK)
    for j in range(CHUNK):
        pr, pi = pwc[j + 1]
        cols = slice(j * LANES, (j + 1) * LANES)
        wy_ref[0:SUPER_STATES, cols] = (cre * pr - cim * pi).astype(BF16)
        wy_ref[SUPER_STATES:, cols] = (-(cre * pi + cim * pr)).astype(BF16)


def _prep(lam_re, lam_im, log_step, b_re, b_im, c_re, c_im):
    gl = S5_GROUPS // N_SUPER
    eye = jnp.eye(gl, dtype=bool)

    def rows(v):
        return v.reshape(N_SUPER, 1, SUPER_STATES)

    def cols(v):
        return v.reshape(N_SUPER, SUPER_STATES, 1)

    ls = jnp.broadcast_to(log_step[:, None], (S5_GROUPS, S5_STATE))
    b4r = b_re.reshape(N_SUPER, gl, S5_STATE, S5_GROUP)
    b4i = b_im.reshape(N_SUPER, gl, S5_STATE, S5_GROUP)
    c4r = c_re.reshape(N_SUPER, gl, S5_GROUP, S5_STATE)
    c4i = c_im.reshape(N_SUPER, gl, S5_GROUP, S5_STATE)

    def bblk(b4):
        t = jnp.transpose(b4, (0, 1, 3, 2))[:, :, :, None, :]
        t = jnp.where(eye[None, :, None, :, None], t, 0.0)
        return t.reshape(N_SUPER, LANES, SUPER_STATES)

    def cblk(c4):
        t = jnp.transpose(c4, (0, 1, 3, 2))[:, :, :, None, :]
        t = jnp.where(eye[None, :, None, :, None], t, 0.0)
        return t.reshape(N_SUPER, SUPER_STATES, LANES)

    row_spec = pl.BlockSpec((None, 1, SUPER_STATES), lambda s: (s, 0, 0))
    col_spec = pl.BlockSpec((None, SUPER_STATES, 1), lambda s: (s, 0, 0))
    b_spec = pl.BlockSpec((None, LANES, SUPER_STATES), lambda s: (s, 0, 0))
    c_spec = pl.BlockSpec((None, SUPER_STATES, LANES), lambda s: (s, 0, 0))
    k = CHUNK * LANES
    return pl.pallas_call(
        _prep_kernel,
        grid=(N_SUPER,),
        in_specs=[row_spec] * 3 + [col_spec] * 3 + [b_spec] * 2 + [c_spec] * 2,
        out_specs=[
            pl.BlockSpec((None, k, k + 2 * SUPER_STATES), lambda s: (s, 0, 0)),
            pl.BlockSpec((None, 2 * SUPER_STATES, k), lambda s: (s, 0, 0)),
            row_spec, row_spec,
        ],
        out_shape=[
            jax.ShapeDtypeStruct((N_SUPER, k, k + 2 * SUPER_STATES), BF16),
            jax.ShapeDtypeStruct((N_SUPER, 2 * SUPER_STATES, k), BF16),
            jax.ShapeDtypeStruct((N_SUPER, 1, SUPER_STATES), F32),
            jax.ShapeDtypeStruct((N_SUPER, 1, SUPER_STATES), F32),
        ],
        compiler_params=pltpu.CompilerParams(vmem_limit_bytes=VMEM_LIMIT),
        name="s5_prep",
    )(rows(lam_re), rows(lam_im), rows(ls), cols(lam_re), cols(lam_im), cols(ls),
      bblk(b4r), bblk(b4i), cblk(c4r), cblk(c4i))


def _stage_a_kernel(x_ref, n1_ref, up_ref, down_ref, mixn_ref, wuv_ref,
                    x1_ref, us_ref, upool_ref):
    x = x_ref[...]
    xn = _rmsnorm(x, n1_ref[...]).astype(BF16)
    x1 = x + 0.5 * _swiglu(xn, up_ref, down_ref)
    x1_ref[...] = x1
    hn = _rmsnorm(x1, mixn_ref[...]).astype(BF16)
    z = _dot(hn, wuv_ref[...])
    for s in range(N_SUPER):
        us_ref[s] = z[:, s * LANES:(s + 1) * LANES]
    upool_ref[...] = z[:, S5_WIDTH:]


def _const_spec(shape):
    zeros = (0,) * len(shape)
    return pl.BlockSpec(shape, lambda *_: zeros, pipeline_mode=pl.Buffered(1))


def _stage_a(x, n1, up, down, mixn, wuv, tm):
    n = x.shape[0]
    return pl.pallas_call(
        _stage_a_kernel,
        grid=(n // tm,),
        in_specs=[
            pl.BlockSpec((tm, D_MODEL), lambda i: (i, 0)),
            _const_spec((1, D_MODEL)),
            _const_spec((D_MODEL, 2 * D_FF)),
            _const_spec((D_FF, D_MODEL)),
            _const_spec((1, D_MODEL)),
            _const_spec((D_MODEL, S5_WIDTH + POOL_WIDTH)),
        ],
        out_specs=[
            pl.BlockSpec((tm, D_MODEL), lambda i: (i, 0)),
            pl.BlockSpec((N_SUPER, tm, LANES), lambda i: (0, i, 0)),
            pl.BlockSpec((tm, POOL_WIDTH), lambda i: (i, 0)),
        ],
        out_shape=[
            jax.ShapeDtypeStruct((n, D_MODEL), F32),
            jax.ShapeDtypeStruct((N_SUPER, n, LANES), F32),
            jax.ShapeDtypeStruct((n, POOL_WIDTH), F32),
        ],
        compiler_params=pltpu.CompilerParams(
            dimension_semantics=("arbitrary",), vmem_limit_bytes=VMEM_LIMIT),
        name="stage_a",
    )(x, n1, up, down, mixn, wuv)


def _stage_b_scan_kernel(u_ref, w1_ref, wy_ref, a8r_ref, a8i_ref,
                         y_ref, hr_out, hi_out, row_scr, x_scr, hin_scr, h_scr):
    nseq = u_ref.shape[0]
    tt = u_ref.shape[1] // CHUNK
    k = CHUNK * LANES
    for b in range(nseq):
        for j in range(CHUNK):
            row_scr[j, pl.ds(b, tt, stride=nseq), :] = u_ref[b, pl.ds(j, tt, stride=CHUNK), :]
    lhs = jnp.concatenate([row_scr[j] for j in range(CHUNK)], axis=1).astype(BF16)
    res = _dot(lhs, w1_ref[...])
    x_scr[...] = res[:, k:]

    @pl.when(pl.program_id(1) == 0)
    def _():
        h_scr[...] = jnp.zeros_like(h_scr)

    a8r = jnp.broadcast_to(a8r_ref[...], (nseq, SUPER_STATES))
    a8i = jnp.broadcast_to(a8i_ref[...], (nseq, SUPER_STATES))

    def step(c, carry):
        hr, hi = carry
        r = pl.multiple_of(c * nseq, nseq)
        hin_scr[pl.ds(r, nseq), 0:SUPER_STATES] = hr
        hin_scr[pl.ds(r, nseq), SUPER_STATES:] = hi
        xr = x_scr[pl.ds(r, nseq), 0:SUPER_STATES]
        xi = x_scr[pl.ds(r, nseq), SUPER_STATES:]
        return (a8r * hr - a8i * hi + xr, a8r * hi + a8i * hr + xi)

    hr, hi = lax.fori_loop(0, tt, step, (h_scr[0], h_scr[1]), unroll=4)
    h_scr[0] = hr
    h_scr[1] = hi
    hr_out[...] = hr
    hi_out[...] = hi
    y = res[:, :k] + _dot(hin_scr[...].astype(BF16), wy_ref[...])
    for j in range(CHUNK):
        row_scr[j] = y[:, j * LANES:(j + 1) * LANES]
    for b in range(nseq):
        for j in range(CHUNK):
            y_ref[b, pl.ds(j, tt, stride=CHUNK), :] = row_scr[j, pl.ds(b, tt, stride=nseq), :]


def _stage_b_step_kernel(u_ref, hr_ref, hi_ref, w1_ref, wy_ref, a8r_ref, a8i_ref,
                         y_ref, hr_out, hi_out):
    k = CHUNK * LANES
    nseq = u_ref.shape[0] // CHUNK
    lhs = jnp.concatenate(
        [u_ref[pl.ds(j, nseq, stride=CHUNK), :] for j in range(CHUNK)], axis=1).astype(BF16)
    res = _dot(lhs, w1_ref[...])
    hr = hr_ref[...]
    hi = hi_ref[...]
    hin = jnp.concatenate([hr, hi], axis=1).astype(BF16)
    y = res[:, :k] + _dot(hin, wy_ref[...])
    for j in range(CHUNK):
        y_ref[pl.ds(j, nseq, stride=CHUNK), :] = y[:, j * LANES:(j + 1) * LANES]
    a8r = a8r_ref[...]
    a8i = a8i_ref[...]
    hr_out[...] = a8r * hr - a8i * hi + res[:, k:k + SUPER_STATES]
    hi_out[...] = a8r * hi + a8i * hr + res[:, k + SUPER_STATES:]


def _stage_b_specs():
    k = CHUNK * LANES
    w1 = pl.BlockSpec((None, k, k + 2 * SUPER_STATES), lambda s, *_: (s, 0, 0))
    wy = pl.BlockSpec((None, 2 * SUPER_STATES, k), lambda s, *_: (s, 0, 0))
    a8 = pl.BlockSpec((None, 1, SUPER_STATES), lambda s, *_: (s, 0, 0))
    return w1, wy, a8


def _stage_b_prompt(us4, w1, wy, a8r, a8i, tt):
    _, nseq, seq, _ = us4.shape
    rows = nseq * tt
    w1s, wys, a8s = _stage_b_specs()
    u = pl.BlockSpec((None, nseq, tt * CHUNK, LANES), lambda s, t: (s, 0, t, 0))
    st = pl.BlockSpec((nseq, SUPER_STATES), lambda s, t: (0, s))
    return pl.pallas_call(
        _stage_b_scan_kernel,
        grid=(N_SUPER, seq // (tt * CHUNK)),
        in_specs=[u, w1s, wys, a8s, a8s],
        out_specs=[u, st, st],
        out_shape=[
            jax.ShapeDtypeStruct(us4.shape, F32),
            jax.ShapeDtypeStruct((nseq, N_SUPER * SUPER_STATES), F32),
            jax.ShapeDtypeStruct((nseq, N_SUPER * SUPER_STATES), F32),
        ],
        scratch_shapes=[pltpu.VMEM((CHUNK, rows, LANES), F32),
                        pltpu.VMEM((rows, 2 * SUPER_STATES), F32),
                        pltpu.VMEM((rows, 2 * SUPER_STATES), F32),
                        pltpu.VMEM((2, nseq, SUPER_STATES), F32)],
        compiler_params=pltpu.CompilerParams(
            dimension_semantics=("arbitrary", "arbitrary"), vmem_limit_bytes=VMEM_LIMIT),
        name="stage_b_prompt",
    )(us4, w1, wy, a8r, a8i)


def _stage_b_sample(us4, h_re, h_im, w1, wy, a8r, a8i):
    tokens = us4.shape[1]
    rows = tokens // CHUNK
    w1s, wys, a8s = _stage_b_specs()
    st = pl.BlockSpec((rows, SUPER_STATES), lambda s: (0, s))
    u = pl.BlockSpec((None, tokens, LANES), lambda s: (s, 0, 0))
    return pl.pallas_call(
        _stage_b_step_kernel,
        grid=(N_SUPER,),
        in_specs=[u, st, st, w1s, wys, a8s, a8s],
        out_specs=[u, st, st],
        out_shape=[
            jax.ShapeDtypeStruct(us4.shape, F32),
            jax.ShapeDtypeStruct((rows, N_SUPER * SUPER_STATES), F32),
            jax.ShapeDtypeStruct((rows, N_SUPER * SUPER_STATES), F32),
        ],
        compiler_params=pltpu.CompilerParams(
            dimension_semantics=("arbitrary",), vmem_limit_bytes=VMEM_LIMIT),
        name="stage_b_sample",
    )(us4, h_re, h_im, w1, wy, a8r, a8i)


def _mixer_tail(x1, us, ys_pre, pooled, mixn_ref, wg_ref, dskip_ref, wglu_ref, wpool_ref,
                pscale_ref, wout_ref, n2_ref, up_ref, down_ref, fnorm_ref):
    hn = _rmsnorm(x1, mixn_ref[...]).astype(BF16)
    gates = jax.nn.sigmoid(_dot(hn, wg_ref[...]))
    ys = jax.nn.gelu(ys_pre + dskip_ref[...] * us)
    ab = _dot(ys.astype(BF16), wglu_ref[...])
    y_s = ab[:, :D_MODEL] * jax.nn.sigmoid(ab[:, D_MODEL:])
    y_p = jnp.concatenate(
        [_dot(pooled[g].astype(BF16), wpool_ref[g]) for g in range(len(POOL_WINDOWS))], axis=1)
    y_p = y_p * pscale_ref[...]
    merged = gates[:, :D_MODEL] * y_s + gates[:, D_MODEL:] * y_p
    x2 = x1 + _dot(merged.astype(BF16), wout_ref[...])
    xn = _rmsnorm(x2, n2_ref[...]).astype(BF16)
    x3 = x2 + 0.5 * _swiglu(xn, up_ref, down_ref)
    return _rmsnorm(x3, fnorm_ref[...])


def _stage_c_prompt_kernel(tiles_per_seq, x1_ref, us_ref, y_ref, upool_ref, halo_ref, *rest):
    weights, out_ref, pad_scr = rest[:-2], rest[-2], rest[-1]
    tm = x1_ref.shape[0]
    t_in_seq = pl.program_id(0) % tiles_per_seq
    keep = (t_in_seq > 0).astype(F32)
    pad_scr[0:HALO, :] = halo_ref[...] * keep
    v = upool_ref[...]
    pad_scr[HALO:, :] = v
    pos = (t_in_seq * tm + lax.broadcasted_iota(jnp.int32, (tm, 1), 0)).astype(F32)
    pooled = []
    for g, w in enumerate(POOL_WINDOWS):
        lanes = slice(g * POOL_GROUP_W, (g + 1) * POOL_GROUP_W)
        s = pad_scr[HALO:HALO + tm, lanes]
        for k in range(1, w):
            s = s + pad_scr[HALO - k:HALO - k + tm, lanes]
        cnt = jnp.minimum(float(w), pos + 1.0)
        pooled.append(s / cnt - v[:, lanes])
    us = jnp.concatenate([us_ref[s] for s in range(N_SUPER)], axis=1)
    ys_pre = jnp.concatenate([y_ref[s] for s in range(N_SUPER)], axis=1)
    out_ref[...] = _mixer_tail(x1_ref[...], us, ys_pre, pooled, *weights)


def _stage_c_sample_kernel(x1_ref, us_ref, y_ref, upool_ref, buf_ref, *rest):
    weights, out_ref, pad_scr = rest[:-2], rest[-2], rest[-1]
    tm = x1_ref.shape[0]
    nb = tm // CHUNK
    pad_scr[:, 0:HALO, :] = buf_ref[...]
    v = upool_ref[...]
    pad_scr[:, HALO:, :] = v.reshape(nb, CHUNK, POOL_WIDTH)
    pooled = []
    for g, w in enumerate(POOL_WINDOWS):
        lanes = slice(g * POOL_GROUP_W, (g + 1) * POOL_GROUP_W)
        s = pad_scr[:, HALO:HALO + CHUNK, lanes]
        for k in range(1, w):
            s = s + pad_scr[:, HALO - k:HALO - k + CHUNK, lanes]
        pooled.append(s.reshape(tm, POOL_GROUP_W) / float(w) - v[:, lanes])
    us = jnp.concatenate([us_ref[s] for s in range(N_SUPER)], axis=1)
    ys_pre = jnp.concatenate([y_ref[s] for s in range(N_SUPER)], axis=1)
    out_ref[...] = _mixer_tail(x1_ref[...], us, ys_pre, pooled, *weights)


def _stage_c_weight_specs():
    return [
        _const_spec((1, D_MODEL)),
        _const_spec((D_MODEL, 2 * D_MODEL)),
        _const_spec((1, S5_WIDTH)),
        _const_spec((S5_WIDTH, 2 * D_MODEL)),
        _const_spec((len(POOL_WINDOWS), POOL_GROUP_W, POOL_OUT_W)),
        _const_spec((1, D_MODEL)),
        _const_spec((D_MODEL, D_MODEL)),
        _const_spec((1, D_MODEL)),
        _const_spec((D_MODEL, 2 * D_FF)),
        _const_spec((D_FF, D_MODEL)),
        _const_spec((1, D_MODEL)),
    ]


def _stage_c_prompt(x1, us4, y4, upool, weights, seq, tm):
    n = x1.shape[0]
    tiles_per_seq = seq // tm
    halo_blocks = tm // HALO
    row = pl.BlockSpec((tm, D_MODEL), lambda i: (i, 0))
    lane4 = pl.BlockSpec((N_SUPER, tm, LANES), lambda i: (0, i, 0))
    return pl.pallas_call(
        functools.partial(_stage_c_prompt_kernel, tiles_per_seq),
        grid=(n // tm,),
        in_specs=[row, lane4, lane4,
                  pl.BlockSpec((tm, POOL_WIDTH), lambda i: (i, 0)),
                  pl.BlockSpec((HALO, POOL_WIDTH),
                               lambda i: (jnp.maximum(i * halo_blocks - 1, 0), 0)),
                  ] + _stage_c_weight_specs(),
        out_specs=row,
        out_shape=jax.ShapeDtypeStruct((n, D_MODEL), F32),
        scratch_shapes=[pltpu.VMEM((HALO + tm, POOL_WIDTH), F32)],
        compiler_params=pltpu.CompilerParams(
            dimension_semantics=("arbitrary",), vmem_limit_bytes=VMEM_LIMIT),
        name="stage_c_prompt",
    )(x1, us4, y4, upool, upool, *weights)


def _stage_c_sample(x1, us4, y4, upool, buf16, weights, tm):
    n = x1.shape[0]
    nb = tm // CHUNK
    row = pl.BlockSpec((tm, D_MODEL), lambda i: (i, 0))
    lane4 = pl.BlockSpec((N_SUPER, tm, LANES), lambda i: (0, i, 0))
    return pl.pallas_call(
        _stage_c_sample_kernel,
        grid=(n // tm,),
        in_specs=[row, lane4, lane4,
                  pl.BlockSpec((tm, POOL_WIDTH), lambda i: (i, 0)),
                  pl.BlockSpec((nb, HALO, POOL_WIDTH), lambda i: (i, 0, 0)),
                  ] + _stage_c_weight_specs(),
        out_specs=row,
        out_shape=jax.ShapeDtypeStruct((n, D_MODEL), F32),
        scratch_shapes=[pltpu.VMEM((nb, HALO + CHUNK, POOL_WIDTH), F32)],
        compiler_params=pltpu.CompilerParams(
            dimension_semantics=("arbitrary",), vmem_limit_bytes=VMEM_LIMIT),
        name="stage_c_sample",
    )(x1, us4, y4, upool, buf16, *weights)


TM_A = 512
TM_C = 512
TT_B = 64


def kernel(x_prompt, x_sample, state_ssm_re, state_ssm_im, state_pool, ffn1_norm, ffn1_up,
           ffn1_down, mix_norm, w_in, lam_re, lam_im, log_step, b_re, b_im, c_re, c_im, d_skip,
           w_glu, w_pool, pool_scale, w_out, ffn2_norm, ffn2_up, ffn2_down, final_norm):
    depth = ffn1_norm.shape[0]
    assert depth == 1, "single-layer trunk"
    bp, seq, _ = x_prompt.shape
    bs, dseq, _ = x_sample.shape
    assert dseq == CHUNK and seq % CHUNK == 0
    l = 0
    row = lambda v: v.reshape(1, -1)
    n1, mixn, n2, fnorm = row(ffn1_norm[l]), row(mix_norm[l]), row(ffn2_norm[l]), row(final_norm)
    up1, down1 = ffn1_up[l].astype(BF16), ffn1_down[l].astype(BF16)
    up2, down2 = ffn2_up[l].astype(BF16), ffn2_down[l].astype(BF16)
    w_in_b = w_in[l].astype(BF16)
    wuv, wg = w_in_b[:, :S5_WIDTH + POOL_WIDTH], w_in_b[:, S5_WIDTH + POOL_WIDTH:]
    c_weights = (mixn, wg, row(d_skip[l]), w_glu[l].astype(BF16), w_pool[l].astype(BF16),
                 row(pool_scale[l]), w_out[l].astype(BF16), n2, up2, down2, fnorm)

    w1, wy, a8r, a8i = _prep(lam_re[l], lam_im[l], log_step[l], b_re[l], b_im[l], c_re[l], c_im[l])

    assert bp == SUBLANES and seq % (TT_B * CHUNK) == 0
    xp = x_prompt.reshape(bp * seq, D_MODEL)
    x1p, us4p, upoolp = _stage_a(xp, n1, up1, down1, mixn, wuv, TM_A)
    y4p, hrp, hip = _stage_b_prompt(us4p.reshape(N_SUPER, bp, seq, LANES), w1, wy, a8r, a8i, TT_B)
    yp = _stage_c_prompt(x1p, us4p, y4p.reshape(N_SUPER, bp * seq, LANES), upoolp, c_weights,
                         seq, TM_C)

    xs = x_sample.reshape(bs * dseq, D_MODEL)
    x1s, us4s, upools = _stage_a(xs, n1, up1, down1, mixn, wuv, TM_A)
    h_re = state_ssm_re[l].reshape(bs, S5_GROUPS * S5_STATE)
    h_im = state_ssm_im[l].reshape(bs, S5_GROUPS * S5_STATE)
    y4s, hrs, his = _stage_b_sample(us4s, h_re, h_im, w1, wy, a8r, a8i)
    buf16 = jnp.pad(state_pool[l], ((0, 0), (HALO - POOL_BUF, 0), (0, 0)))
    ys = _stage_c_sample(x1s, us4s, y4s, upools, buf16, c_weights, TM_C)

    st_shape = lambda b: (depth, b, S5_GROUPS, S5_STATE)
    new_pool_p = upoolp.reshape(bp, seq, POOL_WIDTH)[:, seq - POOL_BUF:]
    new_pool_s = jnp.concatenate(
        [state_pool[l], upools.reshape(bs, dseq, POOL_WIDTH)], axis=1)[:, dseq:]
    return (yp.reshape(bp, seq, D_MODEL), ys.reshape(bs, dseq, D_MODEL),
            hrp.reshape(st_shape(bp)), hip.reshape(st_shape(bp)),
            new_pool_p[None].astype(state_pool.dtype),
            hrs.reshape(st_shape(bs)), his.reshape(st_shape(bs)),
            new_pool_s[None].astype(state_pool.dtype))
```

```python
import functools

import jax
import jax.numpy as jnp
from jax import lax
from jax.experimental import pallas as pl
from jax.experimental.pallas import tpu as pltpu

D_MODEL = 1024
D_FF = 2816
S5_WIDTH = 512
S5_GROUPS = 32
S5_GROUP = 16
S5_STATE = 64
POOL_WIDTH = 512
POOL_WINDOWS = (2, 4, 8, 16)
POOL_GROUP_W = 128
POOL_OUT_W = 256
POOL_BUF = 15
EPS = 1e-6

LANES = 128
SUBLANES = 8
CHUNK = 8
N_SUPER = S5_WIDTH // LANES
SUPER_STATES = S5_GROUPS // N_SUPER * S5_STATE
HALO = 16
FF_CHUNKS = ((0, 1024), (1024, 1024), (2048, 768))
VMEM_LIMIT = 56 * 1024 * 1024

F32 = jnp.float32
BF16 = jnp.bfloat16


def _dot(a, b):
    return jnp.dot(a, b, preferred_element_type=F32)


def _rmsnorm(x, g):
    return x * lax.rsqrt(jnp.mean(x * x, axis=-1, keepdims=True) + EPS) * g


def _swiglu(xn, up_ref, down_ref):
    acc = None
    for start, size in FF_CHUNKS:
        g = _dot(xn, up_ref[:, start:start + size])
        u = _dot(xn, up_ref[:, D_FF + start:D_FF + start + size])
        act = (g * jax.nn.sigmoid(g) * u).astype(BF16)
        part = _dot(act, down_ref[start:start + size, :])
        acc = part if acc is None else acc + part
    return acc


def _discretise(lr, li, ls):
    dt = jnp.exp(ls)
    mag = jnp.exp(lr * dt)
    ar = mag * jnp.cos(li * dt)
    ai = mag * jnp.sin(li * dt)
    den = lr * lr + li * li
    nr = ar - 1.0
    ni = ai
    cr = (nr * lr + ni * li) / den
    ci = (ni * lr - nr * li) / den
    return ar, ai, cr, ci


def _powers(ar, ai, n):
    out = [(jnp.ones_like(ar), jnp.zeros_like(ai))]
    for _ in range(n):
        pr, pi = out[-1]
        out.append((pr * ar - pi * ai, pr * ai + pi * ar))
    return out


def _split_bf16(x):
    hi = x.astype(BF16)
    return hi, (x - hi.astype(F32)).astype(BF16)


def _prep_kernel(lr_row, li_row, ls_row, b_re, b_im, c_re, c_im,
                 w1_ref, wy_ref, a8r_ref, a8i_ref):
    ar, ai, cr, ci = _discretise(lr_row[...], li_row[...], ls_row[...])
    pw = _powers(ar, ai, CHUNK)
    a8r_ref[...] = pw[CHUNK][0]
    a8i_ref[...] = pw[CHUNK][1]
    bre = b_re[...]
    bim = b_im[...]
    bbr = cr * bre - ci * bim
    bbi = cr * bim + ci * bre
    cre = c_re[...]
    cim = c_im[...]
    c_hi, c_lo = _split_bf16(jnp.concatenate([cre, -cim], axis=0))
    e = []
    for t in range(CHUNK):
        pr, pi = pw[t]
        e.append((bbr * pr - bbi * pi, bbr * pi + bbi * pr))
    zero_blk = jnp.zeros((LANES, LANES), BF16)
    kt = []
    for t in range(CHUNK):
        e_hi, e_lo = _split_bf16(jnp.concatenate(e[t], axis=1))
        k = _dot(e_hi, c_hi) + (_dot(e_hi, c_lo) + _dot(e_lo, c_hi))
        kt.append(k.astype(BF16))
    for i in range(CHUNK):
        r = slice(i * LANES, (i + 1) * LANES)
        for j in range(CHUNK):
            w1_ref[r, j * LANES:(j + 1) * LANES] = kt[j - i] if j >= i else zero_blk
        er, ei = e[CHUNK - 1 - i]
        w1_ref[r, CHUNK * LANES:CHUNK * LANES + SUPER_STATES] = er.astype(BF16)
        w1_ref[r, CHUNK * LANES + SUPER_STATES:] = ei.astype(BF16)
    fill = jnp.zeros((LANES - CHUNK, SUPER_STATES), F32)
    pw_re_t = jnp.concatenate([p[0] for p in pw[1:]] + [fill], axis=0).T
    pw_im_t = jnp.concatenate([p[1] for p in pw[1:]] + [fill], axis=0).T
    for j in range(CHUNK):
        pr = pw_re_t[:, j:j + 1]
        pi = pw_im_t[:, j:j + 1]
        cols = slice(j * LANES, (j + 1) * LANES)
        wy_ref[0:SUPER_STATES, cols] = (cre * pr - cim * pi).astype(BF16)
        wy_ref[SUPER_STATES:, cols] = (-(cre * pi + cim * pr)).astype(BF16)


def _prep(lam_re, lam_im, log_step, b_re, b_im, c_re, c_im):
    gl = S5_GROUPS // N_SUPER
    eye = jnp.eye(gl, dtype=bool)

    def rows(v):
        return v.reshape(N_SUPER, 1, SUPER_STATES)

    ls = jnp.broadcast_to(log_step[:, None], (S5_GROUPS, S5_STATE))
    b4r = b_re.reshape(N_SUPER, gl, S5_STATE, S5_GROUP)
    b4i = b_im.reshape(N_SUPER, gl, S5_STATE, S5_GROUP)
    c4r = c_re.reshape(N_SUPER, gl, S5_GROUP, S5_STATE)
    c4i = c_im.reshape(N_SUPER, gl, S5_GROUP, S5_STATE)

    def bblk(b4):
        t = jnp.transpose(b4, (0, 1, 3, 2))[:, :, :, None, :]
        t = jnp.where(eye[None, :, None, :, None], t, 0.0)
        return t.reshape(N_SUPER, LANES, SUPER_STATES)

    def cblk(c4):
        t = jnp.transpose(c4, (0, 1, 3, 2))[:, :, :, None, :]
        t = jnp.where(eye[None, :, None, :, None], t, 0.0)
        return t.reshape(N_SUPER, SUPER_STATES, LANES)

    row_spec = pl.BlockSpec((None, 1, SUPER_STATES), lambda s: (s, 0, 0))
    b_spec = pl.BlockSpec((None, LANES, SUPER_STATES), lambda s: (s, 0, 0))
    c_spec = pl.BlockSpec((None, SUPER_STATES, LANES), lambda s: (s, 0, 0))
    k = CHUNK * LANES
    return pl.pallas_call(
        _prep_kernel,
        grid=(N_SUPER,),
        in_specs=[row_spec] * 3 + [b_spec] * 2 + [c_spec] * 2,
        out_specs=[
            pl.BlockSpec((None, k, k + 2 * SUPER_STATES), lambda s: (s, 0, 0)),
            pl.BlockSpec((None, 2 * SUPER_STATES, k), lambda s: (s, 0, 0)),
            row_spec, row_spec,
        ],
        out_shape=[
            jax.ShapeDtypeStruct((N_SUPER, k, k + 2 * SUPER_STATES), BF16),
            jax.ShapeDtypeStruct((N_SUPER, 2 * SUPER_STATES, k), BF16),
            jax.ShapeDtypeStruct((N_SUPER, 1, SUPER_STATES), F32),
            jax.ShapeDtypeStruct((N_SUPER, 1, SUPER_STATES), F32),
        ],
        compiler_params=pltpu.CompilerParams(vmem_limit_bytes=VMEM_LIMIT),
        name="s5_prep",
    )(rows(lam_re), rows(lam_im), rows(ls), bblk(b4r), bblk(b4i), cblk(c4r), cblk(c4i))


def _stage_a_kernel(x_ref, n1_ref, up_ref, down_ref, mixn_ref, wuv_ref,
                    x1_ref, us_ref, upool_ref):
    x = x_ref[...]
    xn = _rmsnorm(x, n1_ref[...]).astype(BF16)
    x1 = x + 0.5 * _swiglu(xn, up_ref, down_ref)
    x1_ref[...] = x1
    hn = _rmsnorm(x1, mixn_ref[...]).astype(BF16)
    z = _dot(hn, wuv_ref[...])
    for s in range(N_SUPER):
        us_ref[s] = z[:, s * LANES:(s + 1) * LANES]
    upool_ref[...] = z[:, S5_WIDTH:]


def _const_spec(shape):
    zeros = (0,) * len(shape)
    return pl.BlockSpec(shape, lambda *_: zeros, pipeline_mode=pl.Buffered(1))


def _w_in_spec(col_block):
    return pl.BlockSpec((D_MODEL, D_MODEL), lambda *_: (0, col_block),
                        pipeline_mode=pl.Buffered(1))


def _stage_a(x, n1, up, down, mixn, wuv, tm):
    n = x.shape[0]
    return pl.pallas_call(
        _stage_a_kernel,
        grid=(n // tm,),
        in_specs=[
            pl.BlockSpec((tm, D_MODEL), lambda i: (i, 0)),
            _const_spec((1, D_MODEL)),
            _const_spec((D_MODEL, 2 * D_FF)),
            _const_spec((D_FF, D_MODEL)),
            _const_spec((1, D_MODEL)),
            _w_in_spec(0),
        ],
        out_specs=[
            pl.BlockSpec((tm, D_MODEL), lambda i: (i, 0)),
            pl.BlockSpec((N_SUPER, tm, LANES), lambda i: (0, i, 0)),
            pl.BlockSpec((tm, POOL_WIDTH), lambda i: (i, 0)),
        ],
        out_shape=[
            jax.ShapeDtypeStruct((n, D_MODEL), F32),
            jax.ShapeDtypeStruct((N_SUPER, n, LANES), F32),
            jax.ShapeDtypeStruct((n, POOL_WIDTH), F32),
        ],
        compiler_params=pltpu.CompilerParams(
            dimension_semantics=("arbitrary",), vmem_limit_bytes=VMEM_LIMIT),
        name="stage_a",
    )(x, n1, up, down, mixn, wuv)


def _stage_b_scan_kernel(u_ref, w1_ref, wy_ref, a8r_ref, a8i_ref,
                         y_ref, hr_out, hi_out, row_scr, x_scr, hin_scr, h_scr):
    nseq = u_ref.shape[0]
    tt = u_ref.shape[1] // CHUNK
    k = CHUNK * LANES
    for b in range(nseq):
        for j in range(CHUNK):
            row_scr[j, pl.ds(b, tt, stride=nseq), :] = u_ref[b, pl.ds(j, tt, stride=CHUNK), :]
    lhs = jnp.concatenate([row_scr[j] for j in range(CHUNK)], axis=1).astype(BF16)
    res = _dot(lhs, w1_ref[...])
    x_scr[...] = res[:, k:]

    @pl.when(pl.program_id(1) == 0)
    def _():
        h_scr[...] = jnp.zeros_like(h_scr)

    a8r = jnp.broadcast_to(a8r_ref[...], (nseq, SUPER_STATES))
    a8i = jnp.broadcast_to(a8i_ref[...], (nseq, SUPER_STATES))

    def step(c, carry):
        hr, hi = carry
        r = pl.multiple_of(c * nseq, nseq)
        hin_scr[pl.ds(r, nseq), 0:SUPER_STATES] = hr
        hin_scr[pl.ds(r, nseq), SUPER_STATES:] = hi
        xr = x_scr[pl.ds(r, nseq), 0:SUPER_STATES]
        xi = x_scr[pl.ds(r, nseq), SUPER_STATES:]
        return (a8r * hr - a8i * hi + xr, a8r * hi + a8i * hr + xi)

    hr, hi = lax.fori_loop(0, tt, step, (h_scr[0], h_scr[1]), unroll=4)
    h_scr[0] = hr
    h_scr[1] = hi
    hr_out[...] = hr
    hi_out[...] = hi
    y = res[:, :k] + _dot(hin_scr[...].astype(BF16), wy_ref[...])
    for j in range(CHUNK):
        row_scr[j] = y[:, j * LANES:(j + 1) * LANES]
    for b in range(nseq):
        for j in range(CHUNK):
            y_ref[b, pl.ds(j, tt, stride=CHUNK), :] = row_scr[j, pl.ds(b, tt, stride=nseq), :]


def _stage_b_step_kernel(u_ref, hr_ref, hi_ref, w1_ref, wy_ref, a8r_ref, a8i_ref,
                         y_ref, hr_out, hi_out):
    k = CHUNK * LANES
    nseq = u_ref.shape[0] // CHUNK
    lhs = jnp.concatenate(
        [u_ref[pl.ds(j, nseq, stride=CHUNK), :] for j in range(CHUNK)], axis=1).astype(BF16)
    res = _dot(lhs, w1_ref[...])
    hr = hr_ref[...]
    hi = hi_ref[...]
    hin = jnp.concatenate([hr, hi], axis=1).astype(BF16)
    y = res[:, :k] + _dot(hin, wy_ref[...])
    for j in range(CHUNK):
        y_ref[pl.ds(j, nseq, stride=CHUNK), :] = y[:, j * LANES:(j + 1) * LANES]
    a8r = a8r_ref[...]
    a8i = a8i_ref[...]
    hr_out[...] = a8r * hr - a8i * hi + res[:, k:k + SUPER_STATES]
    hi_out[...] = a8r * hi + a8i * hr + res[:, k + SUPER_STATES:]


def _stage_b_specs():
    k = CHUNK * LANES
    w1 = pl.BlockSpec((None, k, k + 2 * SUPER_STATES), lambda s, *_: (s, 0, 0))
    wy = pl.BlockSpec((None, 2 * SUPER_STATES, k), lambda s, *_: (s, 0, 0))
    a8 = pl.BlockSpec((None, 1, SUPER_STATES), lambda s, *_: (s, 0, 0))
    return w1, wy, a8


def _stage_b_prompt(us4, w1, wy, a8r, a8i, tt):
    _, nseq, seq, _ = us4.shape
    rows = nseq * tt
    w1s, wys, a8s = _stage_b_specs()
    u = pl.BlockSpec((None, nseq, tt * CHUNK, LANES), lambda s, t: (s, 0, t, 0))
    st = pl.BlockSpec((nseq, SUPER_STATES), lambda s, t: (0, s))
    return pl.pallas_call(
        _stage_b_scan_kernel,
        grid=(N_SUPER, seq // (tt * CHUNK)),
        in_specs=[u, w1s, wys, a8s, a8s],
        out_specs=[u, st, st],
        out_shape=[
            jax.ShapeDtypeStruct(us4.shape, F32),
            jax.ShapeDtypeStruct((nseq, N_SUPER * SUPER_STATES), F32),
            jax.ShapeDtypeStruct((nseq, N_SUPER * SUPER_STATES), F32),
        ],
        scratch_shapes=[pltpu.VMEM((CHUNK, rows, LANES), F32),
                        pltpu.VMEM((rows, 2 * SUPER_STATES), F32),
                        pltpu.VMEM((rows, 2 * SUPER_STATES), F32),
                        pltpu.VMEM((2, nseq, SUPER_STATES), F32)],
        compiler_params=pltpu.CompilerParams(
            dimension_semantics=("arbitrary", "arbitrary"), vmem_limit_bytes=VMEM_LIMIT),
        name="stage_b_prompt",
    )(us4, w1, wy, a8r, a8i)


def _stage_b_sample(us4, h_re, h_im, w1, wy, a8r, a8i):
    tokens = us4.shape[1]
    rows = tokens // CHUNK
    w1s, wys, a8s = _stage_b_specs()
    st = pl.BlockSpec((rows, SUPER_STATES), lambda s: (0, s))
    u = pl.BlockSpec((None, tokens, LANES), lambda s: (s, 0, 0))
    return pl.pallas_call(
        _stage_b_step_kernel,
        grid=(N_SUPER,),
        in_specs=[u, st, st, w1s, wys, a8s, a8s],
        out_specs=[u, st, st],
        out_shape=[
            jax.ShapeDtypeStruct(us4.shape, F32),
            jax.ShapeDtypeStruct((rows, N_SUPER * SUPER_STATES), F32),
            jax.ShapeDtypeStruct((rows, N_SUPER * SUPER_STATES), F32),
        ],
        compiler_params=pltpu.CompilerParams(
            dimension_semantics=("arbitrary",), vmem_limit_bytes=VMEM_LIMIT),
        name="stage_b_sample",
    )(us4, h_re, h_im, w1, wy, a8r, a8i)


def _mixer_tail(x1, us, ys_pre, pooled, mixn_ref, wg0_ref, wg1_ref, dskip_ref, wglu_ref, wpool_ref,
                pscale_ref, wout_ref, n2_ref, up_ref, down_ref, fnorm_ref):
    hn = _rmsnorm(x1, mixn_ref[...]).astype(BF16)
    gate_s = jax.nn.sigmoid(_dot(hn, wg0_ref[...]))
    gate_p = jax.nn.sigmoid(_dot(hn, wg1_ref[...]))
    ys = jax.nn.gelu(ys_pre + dskip_ref[...] * us)
    ab = _dot(ys.astype(BF16), wglu_ref[...])
    y_s = ab[:, :D_MODEL] * jax.nn.sigmoid(ab[:, D_MODEL:])
    y_p = jnp.concatenate(
        [_dot(pooled[g].astype(BF16), wpool_ref[g]) for g in range(len(POOL_WINDOWS))], axis=1)
    y_p = y_p * pscale_ref[...]
    merged = gate_s * y_s + gate_p * y_p
    x2 = x1 + _dot(merged.astype(BF16), wout_ref[...])
    xn = _rmsnorm(x2, n2_ref[...]).astype(BF16)
    x3 = x2 + 0.5 * _swiglu(xn, up_ref, down_ref)
    return _rmsnorm(x3, fnorm_ref[...])


def _window_sums(p):
    out = []
    for g, w in enumerate(POOL_WINDOWS):
        s = p[:, g * POOL_GROUP_W:(g + 1) * POOL_GROUP_W]
        span = 1
        while span < w:
            s = s + pltpu.roll(s, span, 0)
            span *= 2
        out.append(s)
    return out


def _stage_c_prompt_kernel(tiles_per_seq, x1_ref, us_ref, y_ref, upool_ref, halo_ref, *rest):
    weights, out_ref, pad_scr = rest[:-2], rest[-2], rest[-1]
    tm = x1_ref.shape[0]
    t_in_seq = pl.program_id(0) % tiles_per_seq
    keep = (t_in_seq > 0).astype(F32)
    pad_scr[0:HALO, :] = halo_ref[...] * keep
    v = upool_ref[...]
    pad_scr[HALO:, :] = v
    pos = (t_in_seq * tm + lax.broadcasted_iota(jnp.int32, (tm, 1), 0)).astype(F32)
    pooled = []
    for g, s in enumerate(_window_sums(pad_scr[...])):
        cnt = jnp.minimum(float(POOL_WINDOWS[g]), pos + 1.0)
        pooled.append(s[HALO:] / cnt - v[:, g * POOL_GROUP_W:(g + 1) * POOL_GROUP_W])
    us = jnp.concatenate([us_ref[s] for s in range(N_SUPER)], axis=1)
    ys_pre = jnp.concatenate([y_ref[s] for s in range(N_SUPER)], axis=1)
    out_ref[...] = _mixer_tail(x1_ref[...], us, ys_pre, pooled, *weights)


def _stage_c_sample_kernel(x1_ref, us_ref, y_ref, upool_ref, buf_ref, *rest):
    weights, out_ref, pad_scr = rest[:-2], rest[-2], rest[-1]
    tm = x1_ref.shape[0]
    nb = tm // CHUNK
    per_seq = HALO + CHUNK
    pad_scr[:, 0:HALO, :] = buf_ref[...]
    v = upool_ref[...]
    pad_scr[:, HALO:, :] = v.reshape(nb, CHUNK, POOL_WIDTH)
    pooled = []
    for g, s in enumerate(_window_sums(pad_scr[...].reshape(nb * per_seq, POOL_WIDTH))):
        s = s.reshape(nb, per_seq, POOL_GROUP_W)[:, HALO:, :].reshape(tm, POOL_GROUP_W)
        pooled.append(s / float(POOL_WINDOWS[g]) - v[:, g * POOL_GROUP_W:(g + 1) * POOL_GROUP_W])
    us = jnp.concatenate([us_ref[s] for s in range(N_SUPER)], axis=1)
    ys_pre = jnp.concatenate([y_ref[s] for s in range(N_SUPER)], axis=1)
    out_ref[...] = _mixer_tail(x1_ref[...], us, ys_pre, pooled, *weights)


def _stage_c_weight_specs():
    return [
        _const_spec((1, D_MODEL)),
        _w_in_spec(1),
        _w_in_spec(2),
        _const_spec((1, S5_WIDTH)),
        _const_spec((S5_WIDTH, 2 * D_MODEL)),
        _const_spec((len(POOL_WINDOWS), POOL_GROUP_W, POOL_OUT_W)),
        _const_spec((1, D_MODEL)),
        _const_spec((D_MODEL, D_MODEL)),
        _const_spec((1, D_MODEL)),
        _const_spec((D_MODEL, 2 * D_FF)),
        _const_spec((D_FF, D_MODEL)),
        _const_spec((1, D_MODEL)),
    ]


def _stage_c_prompt(x1, us4, y4, upool, weights, seq, tm):
    n = x1.shape[0]
    tiles_per_seq = seq // tm
    halo_blocks = tm // HALO
    row = pl.BlockSpec((tm, D_MODEL), lambda i: (i, 0))
    lane4 = pl.BlockSpec((N_SUPER, tm, LANES), lambda i: (0, i, 0))
    return pl.pallas_call(
        functools.partial(_stage_c_prompt_kernel, tiles_per_seq),
        grid=(n // tm,),
        in_specs=[row, lane4, lane4,
                  pl.BlockSpec((tm, POOL_WIDTH), lambda i: (i, 0)),
                  pl.BlockSpec((HALO, POOL_WIDTH),
                               lambda i: (jnp.maximum(i * halo_blocks - 1, 0), 0)),
                  ] + _stage_c_weight_specs(),
        out_specs=row,
        out_shape=jax.ShapeDtypeStruct((n, D_MODEL), F32),
        scratch_shapes=[pltpu.VMEM((HALO + tm, POOL_WIDTH), F32)],
        compiler_params=pltpu.CompilerParams(
            dimension_semantics=("arbitrary",), vmem_limit_bytes=VMEM_LIMIT),
        name="stage_c_prompt",
    )(x1, us4, y4, upool, upool, *weights)


def _stage_c_sample(x1, us4, y4, upool, buf16, weights, tm):
    n = x1.shape[0]
    nb = tm // CHUNK
    row = pl.BlockSpec((tm, D_MODEL), lambda i: (i, 0))
    lane4 = pl.BlockSpec((N_SUPER, tm, LANES), lambda i: (0, i, 0))
    return pl.pallas_call(
        _stage_c_sample_kernel,
        grid=(n // tm,),
        in_specs=[row, lane4, lane4,
                  pl.BlockSpec((tm, POOL_WIDTH), lambda i: (i, 0)),
                  pl.BlockSpec((nb, HALO, POOL_WIDTH), lambda i: (i, 0, 0)),
                  ] + _stage_c_weight_specs(),
        out_specs=row,
        out_shape=jax.ShapeDtypeStruct((n, D_MODEL), F32),
        scratch_shapes=[pltpu.VMEM((nb, HALO + CHUNK, POOL_WIDTH), F32)],
        compiler_params=pltpu.CompilerParams(
            dimension_semantics=("arbitrary",), vmem_limit_bytes=VMEM_LIMIT),
        name="stage_c_sample",
    )(x1, us4, y4, upool, buf16, *weights)


TM_A = 512
TM_C = 512
TT_B = 64


def kernel(x_prompt, x_sample, state_ssm_re, state_ssm_im, state_pool, ffn1_norm, ffn1_up,
           ffn1_down, mix_norm, w_in, lam_re, lam_im, log_step, b_re, b_im, c_re, c_im, d_skip,
           w_glu, w_pool, pool_scale, w_out, ffn2_norm, ffn2_up, ffn2_down, final_norm):
    depth = ffn1_norm.shape[0]
    assert depth == 1, "single-layer trunk"
    bp, seq, _ = x_prompt.shape
    bs, dseq, _ = x_sample.shape
    assert dseq == CHUNK and seq % CHUNK == 0
    l = 0
    row = lambda v: v.reshape(1, -1)
    n1, mixn, n2, fnorm = row(ffn1_norm[l]), row(mix_norm[l]), row(ffn2_norm[l]), row(final_norm)
    up1, down1 = ffn1_up[l].astype(BF16), ffn1_down[l].astype(BF16)
    up2, down2 = ffn2_up[l].astype(BF16), ffn2_down[l].astype(BF16)
    w_in_b = w_in[l].astype(BF16)
    wuv = w_in_b
    c_weights = (mixn, w_in_b, w_in_b, row(d_skip[l]), w_glu[l].astype(BF16), w_pool[l].astype(BF16),
                 row(pool_scale[l]), w_out[l].astype(BF16), n2, up2, down2, fnorm)

    w1, wy, a8r, a8i = _prep(lam_re[l], lam_im[l], log_step[l], b_re[l], b_im[l], c_re[l], c_im[l])

    assert bp == SUBLANES and seq % (TT_B * CHUNK) == 0
    xp = x_prompt.reshape(bp * seq, D_MODEL)
    x1p, us4p, upoolp = _stage_a(xp, n1, up1, down1, mixn, wuv, TM_A)
    y4p, hrp, hip = _stage_b_prompt(us4p.reshape(N_SUPER, bp, seq, LANES), w1, wy, a8r, a8i, TT_B)
    yp = _stage_c_prompt(x1p, us4p, y4p.reshape(N_SUPER, bp * seq, LANES), upoolp, c_weights,
                         seq, TM_C)

    xs = x_sample.reshape(bs * dseq, D_MODEL)
    x1s, us4s, upools = _stage_a(xs, n1, up1, down1, mixn, wuv, TM_A)
    h_re = state_ssm_re[l].reshape(bs, S5_GROUPS * S5_STATE)
    h_im = state_ssm_im[l].reshape(bs, S5_GROUPS * S5_STATE)
    y4s, hrs, his = _stage_b_sample(us4s, h_re, h_im, w1, wy, a8r, a8i)
    buf16 = jnp.pad(state_pool[l], ((0, 0), (HALO - POOL_BUF, 0), (0, 0)))
    ys = _stage_c_sample(x1s, us4s, y4s, upools, buf16, c_weights, TM_C)

    st_shape = lambda b: (depth, b, S5_GROUPS, S5_STATE)
    new_pool_p = upoolp.reshape(bp, seq, POOL_WIDTH)[:, seq - POOL_BUF:]
    new_pool_s = jnp.concatenate(
        [state_pool[l], upools.reshape(bs, dseq, POOL_WIDTH)], axis=1)[:, dseq:]
    return (yp.reshape(bp, seq, D_MODEL), ys.reshape(bs, dseq, D_MODEL),
            hrp.reshape(st_shape(bp)), hip.reshape(st_shape(bp)),
            new_pool_p[None].astype(state_pool.dtype),
            hrs.reshape(st_shape(bs)), his.reshape(st_shape(bs)),
            new_pool_s[None].astype(state_pool.dtype))
```
